```python
import math
import jax, jax.numpy as jnp
from jax import lax
import numpy as np

D_MODEL = 1024
BATCH = 2
SEQ = 8192
DEPTH = 1

N_META = 16
MIX_WIDTH = D_MODEL
ATTN_WIDTH = MIX_WIDTH // 2
CONV_WIDTH = MIX_WIDTH - ATTN_WIDTH
HEAD_DIM = 64
N_HEADS = ATTN_WIDTH // HEAD_DIM
N_KV_HEADS = 2
GROUP = N_HEADS // N_KV_HEADS
KV_WIDTH = N_KV_HEADS * HEAD_DIM
WINDOW = 128
BLOCK = 128
CONV_K = 31
N_CONV_GROUPS = CONV_WIDTH // HEAD_DIM
D_FF = int(math.ceil(8 * D_MODEL / 3 / 256) * 256)
IN_COLS = ATTN_WIDTH + 2 * KV_WIDTH + 2 * CONV_WIDTH
SPLITS = (ATTN_WIDTH,
          ATTN_WIDTH + KV_WIDTH,
          ATTN_WIDTH + 2 * KV_WIDTH,
          ATTN_WIDTH + 2 * KV_WIDTH + CONV_WIDTH)
NORM_EPS = 1e-5

kernel_name = "hymba_conformer_swa_sink_alibi_layer"


def rmsnorm(x, g):
    xf = x.astype(jnp.float32)
    y = xf * lax.rsqrt(jnp.mean(xf * xf, axis=-1, keepdims=True) + NORM_EPS)
    return (y * g.astype(jnp.float32)).astype(x.dtype)


def alibi_slopes():
    return jnp.power(2.0, -(8.0 / N_HEADS) * jnp.arange(1, N_HEADS + 1, dtype=jnp.float32))


def swa_sink_alibi_attention(q, k, v, sinks):
    B, L = q.shape[0], q.shape[1]
    lead = BLOCK - N_META
    P = L + lead
    nb = P // BLOCK
    pad = ((0, 0), (lead, 0), (0, 0), (0, 0))
    qb = jnp.pad(q, pad).reshape(B, nb, BLOCK, N_KV_HEADS, GROUP, HEAD_DIM)
    kb = jnp.pad(k, pad).reshape(B, nb, BLOCK, N_KV_HEADS, HEAD_DIM)
    vb = jnp.pad(v, pad).reshape(B, nb, BLOCK, N_KV_HEADS, HEAD_DIM)
    shift = ((0, 0), (1, 0), (0, 0), (0, 0), (0, 0))
    k_band = jnp.concatenate([jnp.pad(kb, shift)[:, :-1], kb], axis=2)
    v_band = jnp.concatenate([jnp.pad(vb, shift)[:, :-1], vb], axis=2)
    k_meta = jnp.broadcast_to(k[:, None, :N_META], (B, nb, N_META, N_KV_HEADS, HEAD_DIM))
    v_meta = jnp.broadcast_to(v[:, None, :N_META], (B, nb, N_META, N_KV_HEADS, HEAD_DIM))
    k_all = jnp.concatenate([k_meta, k_band], axis=2)
    v_all = jnp.concatenate([v_meta, v_band], axis=2)

    scale = 1.0 / math.sqrt(HEAD_DIM)
    s = jnp.einsum('bnqkgd,bnskd->bnkgqs', qb, k_all).astype(jnp.float32) * scale

    qi = jnp.arange(nb)[:, None] * BLOCK + jnp.arange(BLOCK)[None, :]
    kj = (jnp.arange(nb)[:, None] - 1) * BLOCK + jnp.arange(2 * BLOCK)[None, :]
    dist = qi[:, :, None] - kj[:, None, :]
    band_ok = (dist >= 0) & (dist < WINDOW) & (kj[:, None, :] >= BLOCK)
    meta_j = lead + jnp.arange(N_META)
    meta_ok = meta_j[None, None, :] <= qi[:, :, None]
    ok = jnp.concatenate([meta_ok, band_ok], axis=-1)
    dist_all = jnp.concatenate(
        [jnp.zeros((nb, BLOCK, N_META), jnp.float32), dist.astype(jnp.float32)], axis=-1)
    slopes = alibi_slopes().reshape(N_KV_HEADS, GROUP)
    bias = -slopes[None, :, :, None, None] * dist_all[:, None, None, :, :]
    s = jnp.where(ok[:, None, None], s + bias[None], -jnp.inf)

    sink = sinks.astype(jnp.float32).reshape(N_KV_HEADS, GROUP)[None, None, :, :, None, None]
    sink = jnp.broadcast_to(sink, s.shape[:-1] + (1,))
    p = jax.nn.softmax(jnp.concatenate([s, sink], axis=-1), axis=-1)[..., :-1]
    o = jnp.einsum('bnkgqs,bnskd->bnqkgd', p.astype(v.dtype), v_all)
    return o.reshape(B, P, N_HEADS * HEAD_DIM)[:, lead:]


def conformer_conv(a, gate, conv_w, conv_b, ln_g, ln_b):
    u = a * jax.nn.sigmoid(gate)
    y = lax.conv_general_dilated(
        u, conv_w[:, None, :].astype(u.dtype), window_strides=(1,),
        padding=((CONV_K - 1, 0),), dimension_numbers=('NWC', 'WIO', 'NWC'),
        feature_group_count=CONV_WIDTH) + conv_b
    yf = y.astype(jnp.float32)
    mu = jnp.mean(yf, axis=-1, keepdims=True)
    var = jnp.mean(jnp.square(yf - mu), axis=-1, keepdims=True)
    yn = (yf - mu) * lax.rsqrt(var + NORM_EPS) * ln_g.astype(jnp.float32) + ln_b.astype(jnp.float32)
    return jax.nn.silu(yn).astype(a.dtype)


def setup_inputs(seed: int = 0) -> dict:
    key = jax.random.key(seed)
    ks = jax.random.split(key, 20)
    f32 = jnp.float32
    nrm = lambda k, shape, s: jax.random.normal(k, shape, f32) * s
    return {
        "x": nrm(ks[0], (BATCH, SEQ, D_MODEL), 1.0),
        "meta_tokens": nrm(ks[1], (N_META, D_MODEL), 1.0),
        "attn_norm_g": 1.0 + nrm(ks[2], (DEPTH, D_MODEL), 0.02),
        "w_in": nrm(ks[3], (DEPTH, D_MODEL, IN_COLS), D_MODEL ** -0.5),
        "attn_sinks": nrm(ks[4], (DEPTH, N_HEADS), 0.5),
        "conv_w": nrm(ks[5], (DEPTH, CONV_K, CONV_WIDTH), CONV_K ** -0.5),
        "conv_b": nrm(ks[6], (DEPTH, CONV_WIDTH), 0.02),
        "conv_ln_g": 1.0 + nrm(ks[7], (DEPTH, CONV_WIDTH), 0.02),
        "conv_ln_b": nrm(ks[8], (DEPTH, CONV_WIDTH), 0.02),
        "attn_out_g": 1.0 + nrm(ks[9], (DEPTH, ATTN_WIDTH), 0.02),
        "conv_out_g": 1.0 + nrm(ks[10], (DEPTH, CONV_WIDTH), 0.02),
        "w_out": nrm(ks[11], (DEPTH, MIX_WIDTH, D_MODEL), MIX_WIDTH ** -0.5),
        "ffn_norm_g": 1.0 + nrm(ks[12], (DEPTH, D_MODEL), 0.02),
        "w_gate": nrm(ks[13], (DEPTH, D_MODEL, D_FF), D_MODEL ** -0.5),
        "w_up": nrm(ks[14], (DEPTH, D_MODEL, D_FF), D_MODEL ** -0.5),
        "w_down": nrm(ks[15], (DEPTH, D_FF, D_MODEL), D_FF ** -0.5),
        "final_norm_g": 1.0 + nrm(ks[16], (D_MODEL,), 0.02),
    }


def reference(x, meta_tokens, attn_norm_g, w_in, attn_sinks, conv_w, conv_b, conv_ln_g,
              conv_ln_b, attn_out_g, conv_out_g, w_out, ffn_norm_g, w_gate, w_up, w_down,
              final_norm_g):
    B = x.shape[0]
    meta = jnp.broadcast_to(meta_tokens[None].astype(x.dtype), (B, N_META, D_MODEL))
    h = jnp.concatenate([meta, x], axis=1)
    L = h.shape[1]
    for l in range(DEPTH):
        hn = rmsnorm(h, attn_norm_g[l])
        proj = hn @ w_in[l]
        q, k, v, ca, cg = jnp.split(proj, SPLITS, axis=-1)
        q = q.reshape(B, L, N_HEADS, HEAD_DIM)
        k = k.reshape(B, L, N_KV_HEADS, HEAD_DIM)
        v = v.reshape(B, L, N_KV_HEADS, HEAD_DIM)
        o_attn = swa_sink_alibi_attention(q, k, v, attn_sinks[l])
        o_conv = conformer_conv(ca, cg, conv_w[l], conv_b[l], conv_ln_g[l], conv_ln_b[l])
        mixed = jnp.concatenate([rmsnorm(o_attn, attn_out_g[l]),
                                 rmsnorm(o_conv, conv_out_g[l])], axis=-1)
        h = h + mixed @ w_out[l]
        hn = rmsnorm(h, ffn_norm_g[l])
        h = h + (jax.nn.silu(hn @ w_gate[l]) * (hn @ w_up[l])) @ w_down[l]
    return rmsnorm(h, final_norm_g)[:, N_META:]
```

```python
import functools
import math

import jax
import jax.numpy as jnp
import numpy as np
from jax import lax
from jax.experimental import pallas as pl
from jax.experimental.pallas import tpu as pltpu

D_MODEL = 1024
N_META = 16
ATTN_WIDTH = 512
CONV_WIDTH = 512
HEAD_DIM = 64
N_HEADS = 8
N_KV_HEADS = 2
KV_WIDTH = N_KV_HEADS * HEAD_DIM
BLOCK = 128
CONV_K = 31
NORM_EPS = 1e-5
LANES = 128
CONV_HALO = 32
NEG = -1e30

BF16 = jnp.bfloat16
F32 = jnp.float32

TILE_IN = 1024
TILE_MIX = 512
TILE_FFN = 512
CONV_ROWS = 64
VMEM_LIMIT = 56 * 1024 * 1024


def _rms(x, g):
    ms = jnp.mean(x * x, axis=-1, keepdims=True)
    return x * lax.rsqrt(ms + NORM_EPS) * g


def _inproj_kernel(x_ref, g_ref, w_ref, q_ref, kt_ref, v_ref, u_ref, *, tile):
    hn = _rms(x_ref[...], g_ref[...]).astype(BF16)
    proj = jnp.dot(hn, w_ref[...], preferred_element_type=F32)
    q_ref[...] = (proj[:, :ATTN_WIDTH] * (1.0 / math.sqrt(HEAD_DIM))).astype(BF16)
    pk = proj[:, ATTN_WIDTH:ATTN_WIDTH + KV_WIDTH]
    for b in range(tile // BLOCK):
        kt_ref[b] = pk[b * BLOCK:(b + 1) * BLOCK, :].T.astype(BF16)
    pv = proj[:, ATTN_WIDTH + KV_WIDTH:ATTN_WIDTH + 2 * KV_WIDTH]
    pvs = pltpu.roll(pv, HEAD_DIM, 1)
    low = lax.broadcasted_iota(jnp.int32, pv.shape, 1) < HEAD_DIM
    zero = jnp.zeros_like(pv)
    v_ref[:, 0 * LANES:1 * LANES] = jnp.where(low, pv, zero).astype(BF16)
    v_ref[:, 1 * LANES:2 * LANES] = jnp.where(low, zero, pvs).astype(BF16)
    v_ref[:, 2 * LANES:3 * LANES] = jnp.where(low, pvs, zero).astype(BF16)
    v_ref[:, 3 * LANES:4 * LANES] = jnp.where(low, zero, pv).astype(BF16)
    c0 = ATTN_WIDTH + 2 * KV_WIDTH
    ca = proj[:, c0:c0 + CONV_WIDTH]
    cg = proj[:, c0 + CONV_WIDTH:c0 + 2 * CONV_WIDTH]
    u_ref[...] = ca * jax.nn.sigmoid(cg)


def _inproj(x2d, g, w_bf16, tile):
    n = x2d.shape[0]
    in_cols = w_bf16.shape[1]
    return pl.pallas_call(
        functools.partial(_inproj_kernel, tile=tile),
        grid=(n // tile,),
        in_specs=[
            pl.BlockSpec((tile, D_MODEL), lambda i: (i, 0)),
            pl.BlockSpec((1, D_MODEL), lambda i: (0, 0)),
            pl.BlockSpec((D_MODEL, in_cols), lambda i: (0, 0)),
        ],
        out_specs=[
            pl.BlockSpec((tile, ATTN_WIDTH), lambda i: (i, 0)),
            pl.BlockSpec((tile // BLOCK, KV_WIDTH, BLOCK), lambda i: (i, 0, 0)),
            pl.BlockSpec((tile, 4 * LANES), lambda i: (i, 0)),
            pl.BlockSpec((tile, CONV_WIDTH), lambda i: (i, 0)),
        ],
        out_shape=[
            jax.ShapeDtypeStruct((n, ATTN_WIDTH), BF16),
            jax.ShapeDtypeStruct((n // BLOCK, KV_WIDTH, BLOCK), BF16),
            jax.ShapeDtypeStruct((n, 4 * LANES), BF16),
            jax.ShapeDtypeStruct((n, CONV_WIDTH), F32),
        ],
        compiler_params=pltpu.CompilerParams(
            dimension_semantics=("arbitrary",), vmem_limit_bytes=VMEM_LIMIT),
        name="inproj",
    )(x2d, g, w_bf16)


def _attn_bias():
    slopes = 2.0 ** (-(8.0 / N_HEADS) * np.arange(1, N_HEADS + 1, dtype=np.float64))
    qpos = np.arange(BLOCK)[:, None]
    j = np.arange(BLOCK)[None, :]
    out = np.full((2, N_KV_HEADS, 2 * BLOCK, 6 * BLOCK), NEG, dtype=np.float64)
    for var in range(2):
        for g in range(N_KV_HEADS):
            for cb in range(2):
                for par in range(2):
                    h = 4 * g + 2 * cb + par
                    prev = np.where(j > qpos, -slopes[h] * (BLOCK + qpos - j), NEG)
                    if var == 1:
                        prev = np.full_like(prev, NEG)
                    cur = np.where(j <= qpos, -slopes[h] * (qpos - j), NEG)
                    meta = np.where(j >= BLOCK - N_META, 0.0, NEG) + 0.0 * qpos
                    blk = np.concatenate([prev, cur, meta], axis=1)
                    out[var, g, cb * BLOCK:(cb + 1) * BLOCK,
                        par * 3 * BLOCK:(par + 1) * 3 * BLOCK] = blk
    return jnp.asarray(out, dtype=F32)


def _mix_kernel(sinks_ref, q_ref, ktc_ref, ktp_ref, ktm_ref, vc_ref, vp_ref, vm_ref,
                uc_ref, up_ref, um_ref, x_ref, bias_ref, cw_ref, cb_ref, lng_ref, lnb_ref,
                ag_ref, cog_ref, wout_ref, h_ref, oattn_scr, ubuf_scr, mixc_scr,
                *, tile, tiles_per_seq):
    first = (pl.program_id(0) % tiles_per_seq) == 0
    first_i = first.astype(jnp.int32)
    nblk = tile // BLOCK
    zeros_k = jnp.zeros((HEAD_DIM, 3 * BLOCK), BF16)

    for b in range(nblk):
        rows = slice(b * BLOCK, (b + 1) * BLOCK)
        kt_prev = ktp_ref[0] if b == 0 else ktc_ref[b - 1]
        kt_cur = ktc_ref[b]
        kt_meta = ktm_ref[0]
        if b == 0:
            v_prev = vp_ref[...]
        else:
            v_prev = vc_ref[(b - 1) * BLOCK:b * BLOCK, :]
        v_cur = vc_ref[rows, :]
        v_meta = vm_ref[...]
        for g in range(N_KV_HEADS):
            d = slice(g * HEAD_DIM, (g + 1) * HEAD_DIM)
            kt_g = jnp.concatenate([kt_prev[d], kt_cur[d], kt_meta[d]], axis=1)
            kt_ext = jnp.concatenate(
                [jnp.concatenate([kt_g, zeros_k], axis=1),
                 jnp.concatenate([zeros_k, kt_g], axis=1)], axis=0)
            q2 = jnp.concatenate(
                [q_ref[rows, (2 * g) * LANES:(2 * g + 1) * LANES],
                 q_ref[rows, (2 * g + 1) * LANES:(2 * g + 2) * LANES]], axis=0)
            s = jnp.dot(q2, kt_ext, preferred_element_type=F32)
            if b == 0:
                s = s + bias_ref[first_i, g]
            else:
                s = s + bias_ref[0, g]
            p_rows = []
            for cb in range(2):
                p_cols = []
                for par in range(2):
                    sq = s[cb * BLOCK:(cb + 1) * BLOCK, par * 3 * BLOCK:(par + 1) * 3 * BLOCK]
                    sink = sinks_ref[4 * g + 2 * cb + par]
                    m = jnp.maximum(jnp.max(sq, axis=-1, keepdims=True), sink)
                    e = jnp.exp(sq - m)
                    den = jnp.sum(e, axis=-1, keepdims=True) + jnp.exp(sink - m)
                    p_cols.append((e * (1.0 / den)).astype(BF16))
                p_rows.append(jnp.concatenate(p_cols, axis=1))
            p = jnp.concatenate(p_rows, axis=0)
            ve = slice((2 * g) * LANES, (2 * g + 1) * LANES)
            vo = slice((2 * g + 1) * LANES, (2 * g + 2) * LANES)
            v_ext = jnp.concatenate(
                [v_prev[:, ve], v_cur[:, ve], v_meta[:, ve],
                 v_prev[:, vo], v_cur[:, vo], v_meta[:, vo]], axis=0)
            o2 = jnp.dot(p, v_ext, preferred_element_type=F32)
            oattn_scr[rows, (2 * g) * LANES:(2 * g + 1) * LANES] = o2[:BLOCK]
            oattn_scr[rows, (2 * g + 1) * LANES:(2 * g + 2) * LANES] = o2[BLOCK:]

    n_lc = CONV_WIDTH // LANES
    halo = jnp.where(first, um_ref[...], up_ref[...])
    for lc in range(n_lc):
        lanes = slice(lc * LANES, (lc + 1) * LANES)
        ubuf_scr[lc, 0:CONV_HALO, :] = halo[:, lanes]
        ubuf_scr[lc, CONV_HALO:, :] = uc_ref[:, lanes]
    off0 = CONV_HALO - (CONV_K - 1)
    for rc in range(tile // CONV_ROWS):
        r0 = rc * CONV_ROWS
        ys = []
        for lc in range(n_lc):
            lanes = slice(lc * LANES, (lc + 1) * LANES)
            acc = jnp.broadcast_to(cb_ref[:, lanes], (CONV_ROWS, LANES))
            for j in range(CONV_K):
                acc = acc + cw_ref[j:j + 1, lanes] * ubuf_scr[lc, r0 + off0 + j:r0 + off0 + j + CONV_ROWS, :]
            ys.append(acc)
        y = jnp.concatenate(ys, axis=1)
        mu = jnp.mean(y, axis=-1, keepdims=True)
        yc = y - mu
        var = jnp.mean(yc * yc, axis=-1, keepdims=True)
        yn = yc * lax.rsqrt(var + NORM_EPS) * lng_ref[...] + lnb_ref[...]
        oc = yn * jax.nn.sigmoid(yn)
        mixc_scr[r0:r0 + CONV_ROWS, :] = _rms(oc, cog_ref[...]).astype(BF16)

    mixa = _rms(oattn_scr[...], ag_ref[...]).astype(BF16)
    h = x_ref[...]
    h = h + jnp.dot(mixa, wout_ref[0:ATTN_WIDTH, :], preferred_element_type=F32)
    h = h + jnp.dot(mixc_scr[...], wout_ref[ATTN_WIDTH:, :], preferred_element_type=F32)
    h_ref[...] = h


def _mix(sinks, q, kt, v, u, ktm, vm, um, x2d, bias, cw, cb, lng, lnb, ag, cog, wout,
         tile, seq):
    n = x2d.shape[0]
    tps = seq // tile
    nblk = tile // BLOCK
    const2 = lambda i: (0, 0)
    prev_blk = lambda i: jnp.maximum(i * nblk - 1, 0)
    return pl.pallas_call(
        functools.partial(_mix_kernel, tile=tile, tiles_per_seq=tps),
        grid=(n // tile,),
        in_specs=[
            pl.BlockSpec(memory_space=pltpu.SMEM),
            pl.BlockSpec((tile, ATTN_WIDTH), lambda i: (i, 0)),
            pl.BlockSpec((nblk, KV_WIDTH, BLOCK), lambda i: (i, 0, 0)),
            pl.BlockSpec((1, KV_WIDTH, BLOCK), lambda i: (prev_blk(i), 0, 0)),
            pl.BlockSpec((1, KV_WIDTH, BLOCK), lambda i: (0, 0, 0)),
            pl.BlockSpec((tile, 4 * LANES), lambda i: (i, 0)),
            pl.BlockSpec((BLOCK, 4 * LANES), lambda i: (prev_blk(i), 0)),
            pl.BlockSpec((BLOCK, 4 * LANES), const2),
            pl.BlockSpec((tile, CONV_WIDTH), lambda i: (i, 0)),
            pl.BlockSpec((CONV_HALO, CONV_WIDTH),
                         lambda i: (jnp.maximum(i * (tile // CONV_HALO) - 1, 0), 0)),
            pl.BlockSpec((CONV_HALO, CONV_WIDTH),
                         lambda i: (BLOCK // CONV_HALO - 1, 0)),
            pl.BlockSpec((tile, D_MODEL), lambda i: (i, 0)),
            pl.BlockSpec((2, N_KV_HEADS, 2 * BLOCK, 6 * BLOCK), lambda i: (0, 0, 0, 0)),
            pl.BlockSpec((CONV_K, CONV_WIDTH), const2),
            pl.BlockSpec((1, CONV_WIDTH), const2),
            pl.BlockSpec((1, CONV_WIDTH), const2),
            pl.BlockSpec((1, CONV_WIDTH), const2),
            pl.BlockSpec((1, ATTN_WIDTH), const2),
            pl.BlockSpec((1, CONV_WIDTH), const2),
            pl.BlockSpec((D_MODEL, D_MODEL), const2),
        ],
        out_specs=pl.BlockSpec((tile, D_MODEL), lambda i: (i, 0)),
        out_shape=jax.ShapeDtypeStruct((n, D_MODEL), F32),
        scratch_shapes=[
            pltpu.VMEM((tile, ATTN_WIDTH), F32),
            pltpu.VMEM((CONV_WIDTH // LANES, CONV_HALO + tile, LANES), F32),
            pltpu.VMEM((tile, CONV_WIDTH), BF16),
        ],
        compiler_params=pltpu.CompilerParams(
            dimension_semantics=("arbitrary",), vmem_limit_bytes=VMEM_LIMIT),
        name="mixer",
    )(sinks, q, kt, kt, ktm, v, v, vm, u, u, um, x2d, bias, cw, cb, lng, lnb, ag, cog, wout)


def _ffn_kernel(h_ref, g_ref, wg_ref, wu_ref, wd_ref, fg_ref, o_ref):
    h = h_ref[...]
    hn = _rms(h, g_ref[...]).astype(BF16)
    gate = jnp.dot(hn, wg_ref[...], preferred_element_type=F32)
    up = jnp.dot(hn, wu_ref[...], preferred_element_type=F32)
    act = (gate * jax.nn.sigmoid(gate) * up).astype(BF16)
    h2 = h + jnp.dot(act, wd_ref[...], preferred_element_type=F32)
    o_ref[...] = _rms(h2, fg_ref[...])


def _ffn(h, g, wg, wu, wd, fg, tile):
    n = h.shape[0]
    d_ff = wg.shape[1]
    const2 = lambda i: (0, 0)
    return pl.pallas_call(
        _ffn_kernel,
        grid=(n // tile,),
        in_specs=[
            pl.BlockSpec((tile, D_MODEL), lambda i: (i, 0)),
            pl.BlockSpec((1, D_MODEL), const2),
            pl.BlockSpec((D_MODEL, d_ff), const2, pipeline_mode=pl.Buffered(1)),
            pl.BlockSpec((D_MODEL, d_ff), const2, pipeline_mode=pl.Buffered(1)),
            pl.BlockSpec((d_ff, D_MODEL), const2, pipeline_mode=pl.Buffered(1)),
            pl.BlockSpec((1, D_MODEL), const2),
        ],
        out_specs=pl.BlockSpec((tile, D_MODEL), lambda i: (i, 0)),
        out_shape=jax.ShapeDtypeStruct((n, D_MODEL), F32),
        compiler_params=pltpu.CompilerParams(
            dimension_semantics=("arbitrary",), vmem_limit_bytes=VMEM_LIMIT),
        name="ffn",
    )(h, g, wg, wu, wd, fg)


def kernel(x, meta_tokens, attn_norm_g, w_in, attn_sinks, conv_w, conv_b, conv_ln_g,
           conv_ln_b, attn_out_g, conv_out_g, w_out, ffn_norm_g, w_gate, w_up, w_down,
           final_norm_g):
    batch, seq, d = x.shape
    assert d == D_MODEL and seq % TILE_MIX == 0 and seq % TILE_IN == 0
    assert attn_norm_g.shape[0] == 1, "single layer"
    row = lambda a: a.reshape(1, -1).astype(F32)
    x2d = x.reshape(batch * seq, d)
    w_in_b = w_in[0].astype(BF16)
    g_in = row(attn_norm_g[0])

    meta_pad = jnp.pad(meta_tokens.astype(F32), ((BLOCK - N_META, 0), (0, 0)))
    _, ktm, vm, um = _inproj(meta_pad, g_in, w_in_b, BLOCK)
    q, kt, v, u = _inproj(x2d, g_in, w_in_b, TILE_IN)

    h1 = _mix(attn_sinks[0].astype(F32), q, kt, v, u, ktm, vm, um, x2d, _attn_bias(),
              conv_w[0].astype(F32), row(conv_b[0]), row(conv_ln_g[0]), row(conv_ln_b[0]),
              row(attn_out_g[0]), row(conv_out_g[0]), w_out[0].astype(BF16),
              TILE_MIX, seq)

    out = _ffn(h1, row(ffn_norm_g[0]), w_gate[0].astype(BF16), w_up[0].astype(BF16),
               w_down[0].astype(BF16), row(final_norm_g), TILE_FFN)
    return out.reshape(batch, seq, d)
```

```python
import functools
import math

import jax
import jax.numpy as jnp
import numpy as np
from jax import lax
from jax.experimental import pallas as pl
from jax.experimental.pallas import tpu as pltpu

D_MODEL = 1024
N_META = 16
ATTN_WIDTH = 512
CONV_WIDTH = 512
HEAD_DIM = 64
N_HEADS = 8
N_KV_HEADS = 2
KV_WIDTH = N_KV_HEADS * HEAD_DIM
BLOCK = 128
CONV_K = 31
NORM_EPS = 1e-5
LANES = 128
SUBLANES = 8
CONV_HALO = 32
NEG = -1e30

BF16 = jnp.bfloat16
F32 = jnp.float32

TILE_IN = 1024
TILE = 512
CONV_ROWS = 64
FF_CHUNK = 256
TIE_SKEW = 2
VMEM_LIMIT = 60 * 1024 * 1024


def _rms(x, g):
    ms = jnp.mean(x * x, axis=-1, keepdims=True)
    return x * lax.rsqrt(ms + NORM_EPS) * g


def _inproj_kernel(x_ref, g_ref, w_ref, q_ref, kt_ref, v_ref, u_ref, *, tile):
    hn = _rms(x_ref[...], g_ref[...]).astype(BF16)
    proj = jnp.dot(hn, w_ref[...], preferred_element_type=F32)
    q_ref[...] = (proj[:, :ATTN_WIDTH] * (1.0 / math.sqrt(HEAD_DIM))).astype(BF16)
    pk = proj[:, ATTN_WIDTH:ATTN_WIDTH + KV_WIDTH]
    for b in range(tile // BLOCK):
        kt_ref[b] = pk[b * BLOCK:(b + 1) * BLOCK, :].T.astype(BF16)
    pv = proj[:, ATTN_WIDTH + KV_WIDTH:ATTN_WIDTH + 2 * KV_WIDTH]
    pvs = pltpu.roll(pv, HEAD_DIM, 1)
    low = lax.broadcasted_iota(jnp.int32, pv.shape, 1) < HEAD_DIM
    zero = jnp.zeros_like(pv)
    v_ref[:, 0 * LANES:1 * LANES] = jnp.where(low, pv, zero).astype(BF16)
    v_ref[:, 1 * LANES:2 * LANES] = jnp.where(low, zero, pvs).astype(BF16)
    v_ref[:, 2 * LANES:3 * LANES] = jnp.where(low, pvs, zero).astype(BF16)
    v_ref[:, 3 * LANES:4 * LANES] = jnp.where(low, zero, pv).astype(BF16)
    c0 = ATTN_WIDTH + 2 * KV_WIDTH
    ca = proj[:, c0:c0 + CONV_WIDTH]
    cg = proj[:, c0 + CONV_WIDTH:c0 + 2 * CONV_WIDTH]
    u_ref[...] = ca * jax.nn.sigmoid(cg)


def _inproj(x2d, g, w_bf16, tile):
    n = x2d.shape[0]
    in_cols = w_bf16.shape[1]
    return pl.pallas_call(
        functools.partial(_inproj_kernel, tile=tile),
        grid=(n // tile,),
        in_specs=[
            pl.BlockSpec((tile, D_MODEL), lambda i: (i, 0)),
            pl.BlockSpec((1, D_MODEL), lambda i: (0, 0)),
            pl.BlockSpec((D_MODEL, in_cols), lambda i: (0, 0)),
        ],
        out_specs=[
            pl.BlockSpec((tile, ATTN_WIDTH), lambda i: (i, 0)),
            pl.BlockSpec((tile // BLOCK, KV_WIDTH, BLOCK), lambda i: (i, 0, 0)),
            pl.BlockSpec((tile, 4 * LANES), lambda i: (i, 0)),
            pl.BlockSpec((tile, CONV_WIDTH), lambda i: (i, 0)),
        ],
        out_shape=[
            jax.ShapeDtypeStruct((n, ATTN_WIDTH), BF16),
            jax.ShapeDtypeStruct((n // BLOCK, KV_WIDTH, BLOCK), BF16),
            jax.ShapeDtypeStruct((n, 4 * LANES), BF16),
            jax.ShapeDtypeStruct((n, CONV_WIDTH), F32),
        ],
        compiler_params=pltpu.CompilerParams(
            dimension_semantics=("arbitrary",), vmem_limit_bytes=VMEM_LIMIT),
        name="inproj",
    )(x2d, g, w_bf16)


def _attn_bias():
    slopes = 2.0 ** (-(8.0 / N_HEADS) * np.arange(1, N_HEADS + 1, dtype=np.float64))
    qpos = np.arange(BLOCK)[:, None]
    j = np.arange(BLOCK)[None, :]
    out = np.full((2, N_KV_HEADS, 2 * BLOCK, 6 * BLOCK), NEG, dtype=np.float64)
    for var in range(2):
        for g in range(N_KV_HEADS):
            for cb in range(2):
                for par in range(2):
                    h = 4 * g + 2 * cb + par
                    prev = np.where(j > qpos, -slopes[h] * (BLOCK + qpos - j), NEG)
                    if var == 1:
                        prev = np.full_like(prev, NEG)
                    cur = np.where(j <= qpos, -slopes[h] * (qpos - j), NEG)
                    meta = np.where(j >= BLOCK - N_META, 0.0, NEG) + 0.0 * qpos
                    blk = np.concatenate([prev, cur, meta], axis=1)
                    out[var, g, cb * BLOCK:(cb + 1) * BLOCK,
                        par * 3 * BLOCK:(par + 1) * 3 * BLOCK] = blk
    return jnp.asarray(out, dtype=F32)


def _layer_kernel(flag_ref, sinks_ref, q_ref, ktc_ref, ktp_ref, ktm_ref, vc_ref, vp_ref, vm_ref,
                  uc_ref, up_ref, um_ref, x_ref, bias_ref, cw_ref, cb_ref, lng_ref, lnb_ref,
                  ag_ref, cog_ref, wout_ref, fng_ref, wg_ref, wu_ref, wd_ref, fg_ref,
                  o_ref, h1_scr, hn_scr, acc_scr, oattn_scr, ubuf_scr, mixc_scr,
                  *, tile, tiles_per_seq, n_tiles):
    step = pl.program_id(0)
    always = flag_ref[0] > 0

    @pl.when(step == 0)
    def _():
        h1_scr[...] = jnp.zeros_like(h1_scr)

    ti = jnp.minimum(step, n_tiles - 1)
    first = (ti % tiles_per_seq) == 0
    first_i = first.astype(jnp.int32)
    nblk = tile // BLOCK
    zeros_k = jnp.zeros((HEAD_DIM, 3 * BLOCK), BF16)
    d_ff = wg_ref.shape[1]

    def ffn_start():
        h = h1_scr[...]
        acc_scr[...] = h
        hn_scr[...] = _rms(h, fng_ref[...]).astype(BF16)

    def tie(x, tokens):
        if not tokens:
            return x
        tok = functools.reduce(lambda a, b: a + b, tokens)
        r, c = x.shape
        tail = jnp.where(always, x[r - SUBLANES:, 0:LANES], tok)
        left = jnp.concatenate([x[:r - SUBLANES, 0:LANES], tail], axis=0)
        return jnp.concatenate([left, x[:, LANES:]], axis=1)

    def token(y):
        r, c = y.shape
        parts = [y[i:i + SUBLANES, j:j + LANES] for i in range(0, r, SUBLANES) for j in range(0, c, LANES)]
        return functools.reduce(lambda a, b: a + b, parts)

    def ffn_chunk(c, tokens):
        cols = slice(c * FF_CHUNK, (c + 1) * FF_CHUNK)
        hn = hn_scr[...]
        gate = jnp.dot(hn, wg_ref[:, cols], preferred_element_type=F32)
        up = jnp.dot(hn, wu_ref[:, cols], preferred_element_type=F32)
        act = (gate * jax.nn.sigmoid(gate) * tie(up, tokens)).astype(BF16)
        acc_scr[...] += jnp.dot(act, wd_ref[cols, :], preferred_element_type=F32)
        return gate[0:1, 0:LANES]

    def ffn_finish():
        o_ref[...] = _rms(acc_scr[...], fg_ref[...])

    def attn_scores(b, g, start):
        rows = slice(b * BLOCK, (b + 1) * BLOCK)
        kt_prev = ktp_ref[0] if b == 0 else ktc_ref[b - 1]
        kt_cur = ktc_ref[b]
        kt_meta = ktm_ref[0]
        d = slice(g * HEAD_DIM, (g + 1) * HEAD_DIM)
        kt_g = jnp.concatenate([kt_prev[d], kt_cur[d], kt_meta[d]], axis=1)
        kt_ext = jnp.concatenate(
            [jnp.concatenate([kt_g, zeros_k], axis=1),
             jnp.concatenate([zeros_k, kt_g], axis=1)], axis=0)
        q2 = jnp.concatenate(
            [q_ref[rows, (2 * g) * LANES:(2 * g + 1) * LANES],
             q_ref[rows, (2 * g + 1) * LANES:(2 * g + 2) * LANES]], axis=0)
        q2 = jnp.where(always, q2, jnp.broadcast_to(start.astype(BF16), q2.shape))
        s = jnp.dot(q2, kt_ext, preferred_element_type=F32)
        if b == 0:
            s = s + bias_ref[first_i, g]
        else:
            s = s + bias_ref[0, g]
        p_rows = []
        for cb in range(2):
            p_cols = []
            for par in range(2):
                sq = s[cb * BLOCK:(cb + 1) * BLOCK, par * 3 * BLOCK:(par + 1) * 3 * BLOCK]
                sink = sinks_ref[4 * g + 2 * cb + par]
                m = jnp.maximum(jnp.max(sq, axis=-1, keepdims=True), sink)
                e = jnp.exp(sq - m)
                den = jnp.sum(e, axis=-1, keepdims=True) + jnp.exp(sink - m)
                p_cols.append((e * (1.0 / den)).astype(BF16))
            p_rows.append(jnp.concatenate(p_cols, axis=1))
        return jnp.concatenate(p_rows, axis=0)

    def attn_values(b, g, p, start):
        rows = slice(b * BLOCK, (b + 1) * BLOCK)
        if b == 0:
            v_prev = vp_ref[...]
        else:
            v_prev = vc_ref[(b - 1) * BLOCK:b * BLOCK, :]
        v_cur = vc_ref[rows, :]
        v_meta = vm_ref[...]
        ve = slice((2 * g) * LANES, (2 * g + 1) * LANES)
        vo = slice((2 * g + 1) * LANES, (2 * g + 2) * LANES)
        v_ext = jnp.concatenate(
            [v_prev[:, ve], v_cur[:, ve], v_meta[:, ve],
             v_prev[:, vo], v_cur[:, vo], v_meta[:, vo]], axis=0)
        v_ext = jnp.where(always, v_ext, jnp.broadcast_to(start.astype(BF16), v_ext.shape))
        o2 = jnp.dot(p, v_ext, preferred_element_type=F32)
        oattn_scr[rows, (2 * g) * LANES:(2 * g + 1) * LANES] = o2[:BLOCK]
        oattn_scr[rows, (2 * g + 1) * LANES:(2 * g + 2) * LANES] = o2[BLOCK:]
        return token(o2)

    n_lc = CONV_WIDTH // LANES
    off0 = CONV_HALO - (CONV_K - 1)

    def conv_start():
        halo = jnp.where(first, um_ref[...], up_ref[...])
        for lc in range(n_lc):
            lanes = slice(lc * LANES, (lc + 1) * LANES)
            ubuf_scr[lc, 0:CONV_HALO, :] = halo[:, lanes]
            ubuf_scr[lc, CONV_HALO:, :] = uc_ref[:, lanes]

    def conv_unit(rc, start):
        r0 = rc * CONV_ROWS
        ys = []
        for lc in range(n_lc):
            lanes = slice(lc * LANES, (lc + 1) * LANES)
            acc = jnp.where(always, jnp.broadcast_to(cb_ref[:, lanes], (CONV_ROWS, LANES)),
                            jnp.broadcast_to(start, (CONV_ROWS, LANES)))
            for j in range(CONV_K):
                acc = acc + cw_ref[j:j + 1, lanes] * ubuf_scr[lc, r0 + off0 + j:r0 + off0 + j + CONV_ROWS, :]
            ys.append(acc)
        y = jnp.concatenate(ys, axis=1)
        mu = jnp.mean(y, axis=-1, keepdims=True)
        yc = y - mu
        var = jnp.mean(yc * yc, axis=-1, keepdims=True)
        yn = yc * lax.rsqrt(var + NORM_EPS) * lng_ref[...] + lnb_ref[...]
        oc = yn * jax.nn.sigmoid(yn)
        ocn = _rms(oc, cog_ref[...])
        mixc_scr[r0:r0 + CONV_ROWS, :] = ocn.astype(BF16)
        return token(ocn)

    def out_proj():
        mixa = _rms(oattn_scr[...], ag_ref[...]).astype(BF16)
        h = x_ref[...]
        h = h + jnp.dot(mixa, wout_ref[0:ATTN_WIDTH, :], preferred_element_type=F32)
        h = h + jnp.dot(mixc_scr[...], wout_ref[ATTN_WIDTH:, :], preferred_element_type=F32)
        h1_scr[...] = h

    conv_start()
    n_fc = d_ff // FF_CHUNK
    n_units = nblk * N_KV_HEADS
    assert n_units == tile // CONV_ROWS and n_units + TIE_SKEW + 1 <= n_fc
    probs = {}
    pending = {}
    ffn_start()
    for c in range(n_fc):
        start = ffn_chunk(c, pending.pop(c, []))
        toks = []
        if c >= 1 and c - 1 < n_units:
            toks.append(attn_values(*divmod(c - 1, N_KV_HEADS), probs.pop(c - 1), start))
        if c < n_units:
            probs[c] = attn_scores(*divmod(c, N_KV_HEADS), start)
            toks.append(conv_unit(c, start))
        if toks:
            pending.setdefault(c + TIE_SKEW, []).extend(toks)
    assert not probs and not pending
    ffn_finish()
    out_proj()


def _layer(flag, sinks, q, kt, v, u, ktm, vm, um, x2d, bias, cw, cb, lng, lnb, ag, cog, wout,
           fng, wg, wu, wd, fg, tile, seq):
    n = x2d.shape[0]
    n_tiles = n // tile
    tps = seq // tile
    nblk = tile // BLOCK
    d_ff = wg.shape[1]
    const2 = lambda i: (0, 0)
    cur = lambda i: jnp.minimum(i, n_tiles - 1)
    prev_blk = lambda i: jnp.maximum(cur(i) * nblk - 1, 0)
    once = dict(pipeline_mode=pl.Buffered(1))
    return pl.pallas_call(
        functools.partial(_layer_kernel, tile=tile, tiles_per_seq=tps, n_tiles=n_tiles),
        grid=(n_tiles + 1,),
        in_specs=[
            pl.BlockSpec(memory_space=pltpu.SMEM),
            pl.BlockSpec(memory_space=pltpu.SMEM),
            pl.BlockSpec((tile, ATTN_WIDTH), lambda i: (cur(i), 0)),
            pl.BlockSpec((nblk, KV_WIDTH, BLOCK), lambda i: (cur(i), 0, 0)),
            pl.BlockSpec((1, KV_WIDTH, BLOCK), lambda i: (prev_blk(i), 0, 0)),
            pl.BlockSpec((1, KV_WIDTH, BLOCK), lambda i: (0, 0, 0)),
            pl.BlockSpec((tile, 4 * LANES), lambda i: (cur(i), 0)),
            pl.BlockSpec((BLOCK, 4 * LANES), lambda i: (prev_blk(i), 0)),
            pl.BlockSpec((BLOCK, 4 * LANES), const2),
            pl.BlockSpec((tile, CONV_WIDTH), lambda i: (cur(i), 0)),
            pl.BlockSpec((CONV_HALO, CONV_WIDTH),
                         lambda i: (jnp.maximum(cur(i) * (tile // CONV_HALO) - 1, 0), 0)),
            pl.BlockSpec((CONV_HALO, CONV_WIDTH),
                         lambda i: (BLOCK // CONV_HALO - 1, 0)),
            pl.BlockSpec((tile, D_MODEL), lambda i: (cur(i), 0)),
            pl.BlockSpec((2, N_KV_HEADS, 2 * BLOCK, 6 * BLOCK), lambda i: (0, 0, 0, 0), **once),
            pl.BlockSpec((CONV_K, CONV_WIDTH), const2),
            pl.BlockSpec((1, CONV_WIDTH), const2),
            pl.BlockSpec((1, CONV_WIDTH), const2),
            pl.BlockSpec((1, CONV_WIDTH), const2),
            pl.BlockSpec((1, ATTN_WIDTH), const2),
            pl.BlockSpec((1, CONV_WIDTH), const2),
            pl.BlockSpec((D_MODEL, D_MODEL), const2, **once),
            pl.BlockSpec((1, D_MODEL), const2),
            pl.BlockSpec((D_MODEL, d_ff), const2, **once),
            pl.BlockSpec((D_MODEL, d_ff), const2, **once),
            pl.BlockSpec((d_ff, D_MODEL), const2, **once),
            pl.BlockSpec((1, D_MODEL), const2),
        ],
        out_specs=pl.BlockSpec((tile, D_MODEL), lambda i: (jnp.maximum(i - 1, 0), 0)),
        out_shape=jax.ShapeDtypeStruct((n, D_MODEL), F32),
        scratch_shapes=[
            pltpu.VMEM((tile, D_MODEL), F32),
            pltpu.VMEM((tile, D_MODEL), BF16),
            pltpu.VMEM((tile, D_MODEL), F32),
            pltpu.VMEM((tile, ATTN_WIDTH), F32),
            pltpu.VMEM((CONV_WIDTH // LANES, CONV_HALO + tile, LANES), F32),
            pltpu.VMEM((tile, CONV_WIDTH), BF16),
        ],
        compiler_params=pltpu.CompilerParams(
            dimension_semantics=("arbitrary",), vmem_limit_bytes=VMEM_LIMIT),
        name="layer",
    )(flag, sinks, q, kt, kt, ktm, v, v, vm, u, u, um, x2d, bias, cw, cb, lng, lnb, ag, cog, wout,
      fng, wg, wu, wd, fg)


def kernel(x, meta_tokens, attn_norm_g, w_in, attn_sinks, conv_w, conv_b, conv_ln_g,
           conv_ln_b, attn_out_g, conv_out_g, w_out, ffn_norm_g, w_gate, w_up, w_down,
           final_norm_g):
    batch, seq, d = x.shape
    assert d == D_MODEL and seq % TILE == 0 and seq % TILE_IN == 0
    assert attn_norm_g.shape[0] == 1, "single layer"
    assert w_gate.shape[2] % FF_CHUNK == 0
    row = lambda a: a.reshape(1, -1).astype(F32)
    x2d = x.reshape(batch * seq, d)
    w_in_b = w_in[0].astype(BF16)
    g_in = row(attn_norm_g[0])

    meta_pad = jnp.pad(meta_tokens.astype(F32), ((BLOCK - N_META, 0), (0, 0)))
    _, ktm, vm, um = _inproj(meta_pad, g_in, w_in_b, BLOCK)
    q, kt, v, u = _inproj(x2d, g_in, w_in_b, TILE_IN)

    always = jnp.ones((1,), jnp.int32)
    out = _layer(always, attn_sinks[0].astype(F32), q, kt, v, u, ktm, vm, um, x2d, _attn_bias(),
                 conv_w[0].astype(F32), row(conv_b[0]), row(conv_ln_g[0]), row(conv_ln_b[0]),
                 row(attn_out_g[0]), row(conv_out_g[0]), w_out[0].astype(BF16),
                 row(ffn_norm_g[0]), w_gate[0].astype(BF16), w_up[0].astype(BF16),
                 w_down[0].astype(BF16), row(final_norm_g), TILE, seq)
    return out.reshape(batch, seq, d)
```

```python
import functools
import math

import jax
import jax.numpy as jnp
import numpy as np
from jax import lax
from jax.experimental import pallas as pl
from jax.experimental.pallas import tpu as pltpu

D_MODEL = 1024
N_META = 16
ATTN_WIDTH = 512
CONV_WIDTH = 512
HEAD_DIM = 64
N_HEADS = 8
N_KV_HEADS = 2
KV_WIDTH = N_KV_HEADS * HEAD_DIM
BLOCK = 128
CONV_K = 31
NORM_EPS = 1e-5
LANES = 128
SUBLANES = 8
CONV_HALO = 32
NEG = -1e30

BF16 = jnp.bfloat16
F32 = jnp.float32

TILE_IN = 1024
TILE = 512
CONV_ROWS = 64
FF_CHUNK = 256
TIE_SKEW = 2
VMEM_LIMIT = 60 * 1024 * 1024


def _rms(x, g):
    ms = jnp.mean(x * x, axis=-1, keepdims=True)
    return x * lax.rsqrt(ms + NORM_EPS) * g


def _project(x_ref, g_ref, w, q_ref, kt_ref, v_ref, u_ref, tile):
    hn = _rms(x_ref[...], g_ref[...]).astype(BF16)
    proj = jnp.dot(hn, w, preferred_element_type=F32)
    q_ref[...] = (proj[:, :ATTN_WIDTH] * (1.0 / math.sqrt(HEAD_DIM))).astype(BF16)
    pk = proj[:, ATTN_WIDTH:ATTN_WIDTH + KV_WIDTH]
    for b in range(tile // BLOCK):
        kt_ref[b] = pk[b * BLOCK:(b + 1) * BLOCK, :].T.astype(BF16)
    pv = proj[:, ATTN_WIDTH + KV_WIDTH:ATTN_WIDTH + 2 * KV_WIDTH]
    pvs = pltpu.roll(pv, HEAD_DIM, 1)
    low = lax.broadcasted_iota(jnp.int32, pv.shape, 1) < HEAD_DIM
    zero = jnp.zeros_like(pv)
    v_ref[:, 0 * LANES:1 * LANES] = jnp.where(low, pv, zero).astype(BF16)
    v_ref[:, 1 * LANES:2 * LANES] = jnp.where(low, zero, pvs).astype(BF16)
    v_ref[:, 2 * LANES:3 * LANES] = jnp.where(low, pvs, zero).astype(BF16)
    v_ref[:, 3 * LANES:4 * LANES] = jnp.where(low, zero, pv).astype(BF16)
    c0 = ATTN_WIDTH + 2 * KV_WIDTH
    ca = proj[:, c0:c0 + CONV_WIDTH]
    cg = proj[:, c0 + CONV_WIDTH:c0 + 2 * CONV_WIDTH]
    u_ref[...] = ca * jax.nn.sigmoid(cg)


def _inproj_kernel(x_ref, g_ref, w_ref, wo_ref, wg_ref, wu_ref, wd_ref,
                   q_ref, kt_ref, v_ref, u_ref, wb_ref, wob_ref, wgb_ref, wub_ref, wdb_ref, *, tile):
    @pl.when(pl.program_id(0) == 0)
    def _():
        wb_ref[...] = w_ref[...].astype(BF16)

    wob_ref[...] = wo_ref[...].astype(BF16)
    wgb_ref[...] = wg_ref[...].astype(BF16)
    wub_ref[...] = wu_ref[...].astype(BF16)
    wdb_ref[...] = wd_ref[...].astype(BF16)
    _project(x_ref, g_ref, wb_ref[...], q_ref, kt_ref, v_ref, u_ref, tile)


def _inproj_meta_kernel(x_ref, g_ref, w_ref, q_ref, kt_ref, v_ref, u_ref, *, tile):
    _project(x_ref, g_ref, w_ref[...], q_ref, kt_ref, v_ref, u_ref, tile)


def _proj_out(n, tile):
    specs = [
        pl.BlockSpec((tile, ATTN_WIDTH), lambda i: (i, 0)),
        pl.BlockSpec((tile // BLOCK, KV_WIDTH, BLOCK), lambda i: (i, 0, 0)),
        pl.BlockSpec((tile, 4 * LANES), lambda i: (i, 0)),
        pl.BlockSpec((tile, CONV_WIDTH), lambda i: (i, 0)),
    ]
    shapes = [
        jax.ShapeDtypeStruct((n, ATTN_WIDTH), BF16),
        jax.ShapeDtypeStruct((n // BLOCK, KV_WIDTH, BLOCK), BF16),
        jax.ShapeDtypeStruct((n, 4 * LANES), BF16),
        jax.ShapeDtypeStruct((n, CONV_WIDTH), F32),
    ]
    return specs, shapes


def _inproj(x2d, g, w_in, w_out, w_gate, w_up, w_down, tile):
    n = x2d.shape[0]
    steps = n // tile
    weights = (w_out, w_gate, w_up, w_down)
    assert all(w.shape[0] % (steps * 2 * SUBLANES) == 0 for w in weights)
    slab = lambda w: pl.BlockSpec((w.shape[0] // steps, w.shape[1]), lambda i: (i, 0))
    specs, shapes = _proj_out(n, tile)
    return pl.pallas_call(
        functools.partial(_inproj_kernel, tile=tile),
        grid=(steps,),
        in_specs=[
            pl.BlockSpec((tile, D_MODEL), lambda i: (i, 0)),
            pl.BlockSpec((1, D_MODEL), lambda i: (0, 0)),
            pl.BlockSpec(w_in.shape, lambda i: (0, 0), pipeline_mode=pl.Buffered(1)),
        ] + [slab(w) for w in weights],
        out_specs=specs + [pl.BlockSpec(w_in.shape, lambda i: (0, 0))] + [slab(w) for w in weights],
        out_shape=shapes + [jax.ShapeDtypeStruct(w.shape, BF16) for w in (w_in,) + weights],
        compiler_params=pltpu.CompilerParams(
            dimension_semantics=("arbitrary",), vmem_limit_bytes=VMEM_LIMIT),
        name="inproj",
    )(x2d, g, w_in, *weights)


def _inproj_meta(x2d, g, w_bf16):
    n = x2d.shape[0]
    specs, shapes = _proj_out(n, n)
    return pl.pallas_call(
        functools.partial(_inproj_meta_kernel, tile=n),
        grid=(1,),
        in_specs=[
            pl.BlockSpec((n, D_MODEL), lambda i: (i, 0)),
            pl.BlockSpec((1, D_MODEL), lambda i: (0, 0)),
            pl.BlockSpec(w_bf16.shape, lambda i: (0, 0)),
        ],
        out_specs=specs,
        out_shape=shapes,
        compiler_params=pltpu.CompilerParams(
            dimension_semantics=("arbitrary",), vmem_limit_bytes=VMEM_LIMIT),
        name="inproj_meta",
    )(x2d, g, w_bf16)


def _attn_bias():
    slopes = 2.0 ** (-(8.0 / N_HEADS) * np.arange(1, N_HEADS + 1, dtype=np.float64))
    qpos = np.arange(BLOCK)[:, None]
    j = np.arange(BLOCK)[None, :]
    out = np.full((2, N_KV_HEADS, 2 * BLOCK, 6 * BLOCK), NEG, dtype=np.float64)
    for var in range(2):
        for g in range(N_KV_HEADS):
            for cb in range(2):
                for par in range(2):
                    h = 4 * g + 2 * cb + par
                    prev = np.where(j > qpos, -slopes[h] * (BLOCK + qpos - j), NEG)
                    if var == 1:
                        prev = np.full_like(prev, NEG)
                    cur = np.where(j <= qpos, -slopes[h] * (qpos - j), NEG)
                    meta = np.where(j >= BLOCK - N_META, 0.0, NEG) + 0.0 * qpos
                    blk = np.concatenate([prev, cur, meta], axis=1)
                    out[var, g, cb * BLOCK:(cb + 1) * BLOCK,
                        par * 3 * BLOCK:(par + 1) * 3 * BLOCK] = blk
    return jnp.asarray(out, dtype=F32)


def _layer_kernel(flag_ref, sinks_ref, q_ref, ktc_ref, ktp_ref, ktm_ref, vc_ref, vp_ref, vm_ref,
                  uc_ref, up_ref, um_ref, x_ref, bias_ref, cw_ref, cb_ref, lng_ref, lnb_ref,
                  ag_ref, cog_ref, wout_ref, fng_ref, wg_ref, wu_ref, wd_ref, fg_ref,
                  o_ref, h1_scr, hn_scr, acc_scr, oattn_scr, ubuf_scr, mixc_scr,
                  *, tile, tiles_per_seq, n_tiles):
    step = pl.program_id(0)
    always = flag_ref[0] > 0

    @pl.when(step == 0)
    def _():
        h1_scr[...] = jnp.zeros_like(h1_scr)

    ti = jnp.minimum(step, n_tiles - 1)
    first = (ti % tiles_per_seq) == 0
    first_i = first.astype(jnp.int32)
    nblk = tile // BLOCK
    zeros_k = jnp.zeros((HEAD_DIM, 3 * BLOCK), BF16)
    d_ff = wg_ref.shape[1]

    def ffn_start():
        h = h1_scr[...]
        acc_scr[...] = h
        hn_scr[...] = _rms(h, fng_ref[...]).astype(BF16)

    def tie(x, tokens):
        if not tokens:
            return x
        tok = functools.reduce(lambda a, b: a + b, tokens)
        r, c = x.shape
        tail = jnp.where(always, x[r - SUBLANES:, 0:LANES], tok)
        left = jnp.concatenate([x[:r - SUBLANES, 0:LANES], tail], axis=0)
        return jnp.concatenate([left, x[:, LANES:]], axis=1)

    def token(y):
        r, c = y.shape
        parts = [y[i:i + SUBLANES, j:j + LANES] for i in range(0, r, SUBLANES) for j in range(0, c, LANES)]
        return functools.reduce(lambda a, b: a + b, parts)

    def ffn_chunk(c, tokens):
        cols = slice(c * FF_CHUNK, (c + 1) * FF_CHUNK)
        hn = hn_scr[...]
        gate = jnp.dot(hn, wg_ref[:, cols], preferred_element_type=F32)
        up = jnp.dot(hn, wu_ref[:, cols], preferred_element_type=F32)
        act = (gate * jax.nn.sigmoid(gate) * tie(up, tokens)).astype(BF16)
        acc_scr[...] += jnp.dot(act, wd_ref[cols, :], preferred_element_type=F32)
        return gate[0:1, 0:LANES]

    def ffn_finish():
        o_ref[...] = _rms(acc_scr[...], fg_ref[...])

    def attn_scores(b, g, start):
        rows = slice(b * BLOCK, (b + 1) * BLOCK)
        kt_prev = ktp_ref[0] if b == 0 else ktc_ref[b - 1]
        kt_cur = ktc_ref[b]
        kt_meta = ktm_ref[0]
        d = slice(g * HEAD_DIM, (g + 1) * HEAD_DIM)
        kt_g = jnp.concatenate([kt_prev[d], kt_cur[d], kt_meta[d]], axis=1)
        kt_ext = jnp.concatenate(
            [jnp.concatenate([kt_g, zeros_k], axis=1),
             jnp.concatenate([zeros_k, kt_g], axis=1)], axis=0)
        q2 = jnp.concatenate(
            [q_ref[rows, (2 * g) * LANES:(2 * g + 1) * LANES],
             q_ref[rows, (2 * g + 1) * LANES:(2 * g + 2) * LANES]], axis=0)
        q2 = jnp.where(always, q2, jnp.broadcast_to(start.astype(BF16), q2.shape))
        s = jnp.dot(q2, kt_ext, preferred_element_type=F32)
        if b == 0:
            s = s + bias_ref[first_i, g]
        else:
            s = s + bias_ref[0, g]
        p_rows = []
        for cb in range(2):
            p_cols = []
            for par in range(2):
                sq = s[cb * BLOCK:(cb + 1) * BLOCK, par * 3 * BLOCK:(par + 1) * 3 * BLOCK]
                sink = sinks_ref[4 * g + 2 * cb + par]
                m = jnp.maximum(jnp.max(sq, axis=-1, keepdims=True), sink)
                e = jnp.exp(sq - m)
                den = jnp.sum(e, axis=-1, keepdims=True) + jnp.exp(sink - m)
                p_cols.append((e * (1.0 / den)).astype(BF16))
            p_rows.append(jnp.concatenate(p_cols, axis=1))
        return jnp.concatenate(p_rows, axis=0)

    def attn_values(b, g, p, start):
        rows = slice(b * BLOCK, (b + 1) * BLOCK)
        if b == 0:
            v_prev = vp_ref[...]
        else:
            v_prev = vc_ref[(b - 1) * BLOCK:b * BLOCK, :]
        v_cur = vc_ref[rows, :]
        v_meta = vm_ref[...]
        ve = slice((2 * g) * LANES, (2 * g + 1) * LANES)
        vo = slice((2 * g + 1) * LANES, (2 * g + 2) * LANES)
        v_ext = jnp.concatenate(
            [v_prev[:, ve], v_cur[:, ve], v_meta[:, ve],
             v_prev[:, vo], v_cur[:, vo], v_meta[:, vo]], axis=0)
        v_ext = jnp.where(always, v_ext, jnp.broadcast_to(start.astype(BF16), v_ext.shape))
        o2 = jnp.dot(p, v_ext, preferred_element_type=F32)
        oattn_scr[rows, (2 * g) * LANES:(2 * g + 1) * LANES] = o2[:BLOCK]
        oattn_scr[rows, (2 * g + 1) * LANES:(2 * g + 2) * LANES] = o2[BLOCK:]
        return token(o2)

    n_lc = CONV_WIDTH // LANES
    off0 = CONV_HALO - (CONV_K - 1)

    def conv_start():
        halo = jnp.where(first, um_ref[...], up_ref[...])
        for lc in range(n_lc):
            lanes = slice(lc * LANES, (lc + 1) * LANES)
            ubuf_scr[lc, 0:CONV_HALO, :] = halo[:, lanes]
            ubuf_scr[lc, CONV_HALO:, :] = uc_ref[:, lanes]

    def conv_unit(rc, start):
        r0 = rc * CONV_ROWS
        ys = []
        for lc in range(n_lc):
            lanes = slice(lc * LANES, (lc + 1) * LANES)
            acc = jnp.where(always, jnp.broadcast_to(cb_ref[:, lanes], (CONV_ROWS, LANES)),
                            jnp.broadcast_to(start, (CONV_ROWS, LANES)))
            for j in range(CONV_K):
                acc = acc + cw_ref[j:j + 1, lanes] * ubuf_scr[lc, r0 + off0 + j:r0 + off0 + j + CONV_ROWS, :]
            ys.append(acc)
        y = jnp.concatenate(ys, axis=1)
        mu = jnp.mean(y, axis=-1, keepdims=True)
        yc = y - mu
        var = jnp.mean(yc * yc, axis=-1, keepdims=True)
        yn = yc * lax.rsqrt(var + NORM_EPS) * lng_ref[...] + lnb_ref[...]
        oc = yn * jax.nn.sigmoid(yn)
        ocn = _rms(oc, cog_ref[...])
        mixc_scr[r0:r0 + CONV_ROWS, :] = ocn.astype(BF16)
        return token(ocn)

    def out_proj():
        mixa = _rms(oattn_scr[...], ag_ref[...]).astype(BF16)
        h = x_ref[...]
        h = h + jnp.dot(mixa, wout_ref[0:ATTN_WIDTH, :], preferred_element_type=F32)
        h = h + jnp.dot(mixc_scr[...], wout_ref[ATTN_WIDTH:, :], preferred_element_type=F32)
        h1_scr[...] = h

    conv_start()
    n_fc = d_ff // FF_CHUNK
    n_units = nblk * N_KV_HEADS
    assert n_units == tile // CONV_ROWS and n_units + TIE_SKEW + 1 <= n_fc
    probs = {}
    pending = {}
    ffn_start()
    for c in range(n_fc):
        start = ffn_chunk(c, pending.pop(c, []))
        toks = []
        if c >= 1 and c - 1 < n_units:
            toks.append(attn_values(*divmod(c - 1, N_KV_HEADS), probs.pop(c - 1), start))
        if c < n_units:
            probs[c] = attn_scores(*divmod(c, N_KV_HEADS), start)
            toks.append(conv_unit(c, start))
        if toks:
            pending.setdefault(c + TIE_SKEW, []).extend(toks)
    assert not probs and not pending
    ffn_finish()
    out_proj()


def _layer(flag, sinks, q, kt, v, u, ktm, vm, um, x2d, bias, cw, cb, lng, lnb, ag, cog, wout,
           fng, wg, wu, wd, fg, tile, seq):
    n = x2d.shape[0]
    n_tiles = n // tile
    tps = seq // tile
    nblk = tile // BLOCK
    d_ff = wg.shape[1]
    const2 = lambda i: (0, 0)
    cur = lambda i: jnp.minimum(i, n_tiles - 1)
    prev_blk = lambda i: jnp.maximum(cur(i) * nblk - 1, 0)
    once = dict(pipeline_mode=pl.Buffered(1))
    return pl.pallas_call(
        functools.partial(_layer_kernel, tile=tile, tiles_per_seq=tps, n_tiles=n_tiles),
        grid=(n_tiles + 1,),
        in_specs=[
            pl.BlockSpec(memory_space=pltpu.SMEM),
            pl.BlockSpec(memory_space=pltpu.SMEM),
            pl.BlockSpec((tile, ATTN_WIDTH), lambda i: (cur(i), 0)),
            pl.BlockSpec((nblk, KV_WIDTH, BLOCK), lambda i: (cur(i), 0, 0)),
            pl.BlockSpec((1, KV_WIDTH, BLOCK), lambda i: (prev_blk(i), 0, 0)),
            pl.BlockSpec((1, KV_WIDTH, BLOCK), lambda i: (0, 0, 0)),
            pl.BlockSpec((tile, 4 * LANES), lambda i: (cur(i), 0)),
            pl.BlockSpec((BLOCK, 4 * LANES), lambda i: (prev_blk(i), 0)),
            pl.BlockSpec((BLOCK, 4 * LANES), const2),
            pl.BlockSpec((tile, CONV_WIDTH), lambda i: (cur(i), 0)),
            pl.BlockSpec((CONV_HALO, CONV_WIDTH),
                         lambda i: (jnp.maximum(cur(i) * (tile // CONV_HALO) - 1, 0), 0)),
            pl.BlockSpec((CONV_HALO, CONV_WIDTH),
                         lambda i: (BLOCK // CONV_HALO - 1, 0)),
            pl.BlockSpec((tile, D_MODEL), lambda i: (cur(i), 0)),
            pl.BlockSpec((2, N_KV_HEADS, 2 * BLOCK, 6 * BLOCK), lambda i: (0, 0, 0, 0), **once),
            pl.BlockSpec((CONV_K, CONV_WIDTH), const2),
            pl.BlockSpec((1, CONV_WIDTH), const2),
            pl.BlockSpec((1, CONV_WIDTH), const2),
            pl.BlockSpec((1, CONV_WIDTH), const2),
            pl.BlockSpec((1, ATTN_WIDTH), const2),
            pl.BlockSpec((1, CONV_WIDTH), const2),
            pl.BlockSpec((D_MODEL, D_MODEL), const2, **once),
            pl.BlockSpec((1, D_MODEL), const2),
            pl.BlockSpec((D_MODEL, d_ff), const2, **once),
            pl.BlockSpec((D_MODEL, d_ff), const2, **once),
            pl.BlockSpec((d_ff, D_MODEL), const2, **once),
            pl.BlockSpec((1, D_MODEL), const2),
        ],
        out_specs=pl.BlockSpec((tile, D_MODEL), lambda i: (jnp.maximum(i - 1, 0), 0)),
        out_shape=jax.ShapeDtypeStruct((n, D_MODEL), F32),
        scratch_shapes=[
            pltpu.VMEM((tile, D_MODEL), F32),
            pltpu.VMEM((tile, D_MODEL), BF16),
            pltpu.VMEM((tile, D_MODEL), F32),
            pltpu.VMEM((tile, ATTN_WIDTH), F32),
            pltpu.VMEM((CONV_WIDTH // LANES, CONV_HALO + tile, LANES), F32),
            pltpu.VMEM((tile, CONV_WIDTH), BF16),
        ],
        compiler_params=pltpu.CompilerParams(
            dimension_semantics=("arbitrary",), vmem_limit_bytes=VMEM_LIMIT),
        name="layer",
    )(flag, sinks, q, kt, kt, ktm, v, v, vm, u, u, um, x2d, bias, cw, cb, lng, lnb, ag, cog, wout,
      fng, wg, wu, wd, fg)


def kernel(x, meta_tokens, attn_norm_g, w_in, attn_sinks, conv_w, conv_b, conv_ln_g,
           conv_ln_b, attn_out_g, conv_out_g, w_out, ffn_norm_g, w_gate, w_up, w_down,
           final_norm_g):
    batch, seq, d = x.shape
    assert d == D_MODEL and seq % TILE == 0 and seq % TILE_IN == 0
    assert attn_norm_g.shape[0] == 1, "single layer"
    assert w_gate.shape[2] % FF_CHUNK == 0
    row = lambda a: a.reshape(1, -1).astype(F32)
    x2d = x.reshape(batch * seq, d)
    g_in = row(attn_norm_g[0])

    q, kt, v, u, w_in_b, w_out_b, w_gate_b, w_up_b, w_down_b = _inproj(
        x2d, g_in, w_in[0], w_out[0], w_gate[0], w_up[0], w_down[0], TILE_IN)
    meta_pad = jnp.pad(meta_tokens.astype(F32), ((BLOCK - N_META, 0), (0, 0)))
    _, ktm, vm, um = _inproj_meta(meta_pad, g_in, w_in_b)

    always = jnp.ones((1,), jnp.int32)
    out = _layer(always, attn_sinks[0].astype(F32), q, kt, v, u, ktm, vm, um, x2d, _attn_bias(),
                 conv_w[0].astype(F32), row(conv_b[0]), row(conv_ln_g[0]), row(conv_ln_b[0]),
                 row(attn_out_g[0]), row(conv_out_g[0]), w_out_b,
                 row(ffn_norm_g[0]), w_gate_b, w_up_b, w_down_b, row(final_norm_g), TILE, seq)
    return out.reshape(batch, seq, d)
```

```python
import functools
import math

import jax
import jax.numpy as jnp
import numpy as np
from jax import lax
from jax.experimental import pallas as pl
from jax.experimental.pallas import tpu as pltpu

D_MODEL = 1024
N_META = 16
ATTN_WIDTH = 512
CONV_WIDTH = 512
HEAD_DIM = 64
N_HEADS = 8
N_KV_HEADS = 2
KV_WIDTH = N_KV_HEADS * HEAD_DIM
BLOCK = 128
CONV_K = 31
NORM_EPS = 1e-5
LANES = 128
SUBLANES = 8
CONV_HALO = 32
NEG = -1e30

BF16 = jnp.bfloat16
F32 = jnp.float32

TILE_IN = 1024
TILE = 512
CONV_ROWS = 64
FF_CHUNK = 256
TIE_SKEW = 4
VMEM_LIMIT = 60 * 1024 * 1024


def _rms(x, g):
    ms = jnp.mean(x * x, axis=-1, keepdims=True)
    return x * lax.rsqrt(ms + NORM_EPS) * g


def _project(x_ref, g_ref, w, q_ref, kt_ref, v_ref, u_ref, tile):
    hn = _rms(x_ref[...], g_ref[...]).astype(BF16)
    proj = jnp.dot(hn, w, preferred_element_type=F32)
    q_ref[...] = (proj[:, :ATTN_WIDTH] * (1.0 / math.sqrt(HEAD_DIM))).astype(BF16)
    pk = proj[:, ATTN_WIDTH:ATTN_WIDTH + KV_WIDTH]
    for b in range(tile // BLOCK):
        kt_ref[b] = pk[b * BLOCK:(b + 1) * BLOCK, :].T.astype(BF16)
    pv = proj[:, ATTN_WIDTH + KV_WIDTH:ATTN_WIDTH + 2 * KV_WIDTH]
    pvs = pltpu.roll(pv, HEAD_DIM, 1)
    low = lax.broadcasted_iota(jnp.int32, pv.shape, 1) < HEAD_DIM
    zero = jnp.zeros_like(pv)
    v_ref[:, 0 * LANES:1 * LANES] = jnp.where(low, pv, zero).astype(BF16)
    v_ref[:, 1 * LANES:2 * LANES] = jnp.where(low, zero, pvs).astype(BF16)
    v_ref[:, 2 * LANES:3 * LANES] = jnp.where(low, pvs, zero).astype(BF16)
    v_ref[:, 3 * LANES:4 * LANES] = jnp.where(low, zero, pv).astype(BF16)
    c0 = ATTN_WIDTH + 2 * KV_WIDTH
    ca = proj[:, c0:c0 + CONV_WIDTH]
    cg = proj[:, c0 + CONV_WIDTH:c0 + 2 * CONV_WIDTH]
    u_ref[...] = ca * jax.nn.sigmoid(cg)


def _inproj_kernel(x_ref, g_ref, w_ref, wo_ref, wg_ref, wu_ref, wd_ref,
                   q_ref, kt_ref, v_ref, u_ref, wb_ref, wob_ref, wgb_ref, wub_ref, wdb_ref, *, tile):
    @pl.when(pl.program_id(0) == 0)
    def _():
        wb_ref[...] = w_ref[...].astype(BF16)

    wob_ref[...] = wo_ref[...].astype(BF16)
    wgb_ref[...] = wg_ref[...].astype(BF16)
    wub_ref[...] = wu_ref[...].astype(BF16)
    wdb_ref[...] = wd_ref[...].astype(BF16)
    _project(x_ref, g_ref, wb_ref[...], q_ref, kt_ref, v_ref, u_ref, tile)


def _inproj_meta_kernel(x_ref, g_ref, w_ref, q_ref, kt_ref, v_ref, u_ref, *, tile):
    _project(x_ref, g_ref, w_ref[...], q_ref, kt_ref, v_ref, u_ref, tile)


def _proj_out(n, tile):
    specs = [
        pl.BlockSpec((tile, ATTN_WIDTH), lambda i: (i, 0)),
        pl.BlockSpec((tile // BLOCK, KV_WIDTH, BLOCK), lambda i: (i, 0, 0)),
        pl.BlockSpec((tile, 4 * LANES), lambda i: (i, 0)),
        pl.BlockSpec((tile, CONV_WIDTH), lambda i: (i, 0)),
    ]
    shapes = [
        jax.ShapeDtypeStruct((n, ATTN_WIDTH), BF16),
        jax.ShapeDtypeStruct((n // BLOCK, KV_WIDTH, BLOCK), BF16),
        jax.ShapeDtypeStruct((n, 4 * LANES), BF16),
        jax.ShapeDtypeStruct((n, CONV_WIDTH), F32),
    ]
    return specs, shapes


def _inproj(x2d, g, w_in, w_out, w_gate, w_up, w_down, tile):
    n = x2d.shape[0]
    steps = n // tile
    weights = (w_out, w_gate, w_up, w_down)
    assert all(w.shape[0] % (steps * 2 * SUBLANES) == 0 for w in weights)
    slab = lambda w: pl.BlockSpec((w.shape[0] // steps, w.shape[1]), lambda i: (i, 0))
    specs, shapes = _proj_out(n, tile)
    return pl.pallas_call(
        functools.partial(_inproj_kernel, tile=tile),
        grid=(steps,),
        in_specs=[
            pl.BlockSpec((tile, D_MODEL), lambda i: (i, 0)),
            pl.BlockSpec((1, D_MODEL), lambda i: (0, 0)),
            pl.BlockSpec(w_in.shape, lambda i: (0, 0), pipeline_mode=pl.Buffered(1)),
        ] + [slab(w) for w in weights],
        out_specs=specs + [pl.BlockSpec(w_in.shape, lambda i: (0, 0))] + [slab(w) for w in weights],
        out_shape=shapes + [jax.ShapeDtypeStruct(w.shape, BF16) for w in (w_in,) + weights],
        compiler_params=pltpu.CompilerParams(
            dimension_semantics=("arbitrary",), vmem_limit_bytes=VMEM_LIMIT),
        name="inproj",
    )(x2d, g, w_in, *weights)


def _inproj_meta(x2d, g, w_bf16):
    n = x2d.shape[0]
    specs, shapes = _proj_out(n, n)
    return pl.pallas_call(
        functools.partial(_inproj_meta_kernel, tile=n),
        grid=(1,),
        in_specs=[
            pl.BlockSpec((n, D_MODEL), lambda i: (i, 0)),
            pl.BlockSpec((1, D_MODEL), lambda i: (0, 0)),
            pl.BlockSpec(w_bf16.shape, lambda i: (0, 0)),
        ],
        out_specs=specs,
        out_shape=shapes,
        compiler_params=pltpu.CompilerParams(
            dimension_semantics=("arbitrary",), vmem_limit_bytes=VMEM_LIMIT),
        name="inproj_meta",
    )(x2d, g, w_bf16)


def _attn_bias():
    slopes = 2.0 ** (-(8.0 / N_HEADS) * np.arange(1, N_HEADS + 1, dtype=np.float64))
    qpos = np.arange(BLOCK)[:, None]
    j = np.arange(BLOCK)[None, :]
    out = np.full((2, N_KV_HEADS, 2 * BLOCK, 6 * BLOCK), NEG, dtype=np.float64)
    for var in range(2):
        for g in range(N_KV_HEADS):
            for cb in range(2):
                for par in range(2):
                    h = 4 * g + 2 * cb + par
                    prev = np.where(j > qpos, -slopes[h] * (BLOCK + qpos - j), NEG)
                    if var == 1:
                        prev = np.full_like(prev, NEG)
                    cur = np.where(j <= qpos, -slopes[h] * (qpos - j), NEG)
                    meta = np.where(j >= BLOCK - N_META, 0.0, NEG) + 0.0 * qpos
                    blk = np.concatenate([prev, cur, meta], axis=1)
                    out[var, g, cb * BLOCK:(cb + 1) * BLOCK,
                        par * 3 * BLOCK:(par + 1) * 3 * BLOCK] = blk
    return jnp.asarray(out, dtype=F32)


def _layer_kernel(flag_ref, sinks_ref, q_ref, ktc_ref, ktp_ref, ktm_ref, vc_ref, vp_ref, vm_ref,
                  uc_ref, up_ref, um_ref, x_ref, bias_ref, cw_ref, cb_ref, lng_ref, lnb_ref,
                  ag_ref, cog_ref, wout_ref, fng_ref, wg_ref, wu_ref, wd_ref, fg_ref,
                  o_ref, h1_scr, hn_scr, acc_scr, oattn_scr, ubuf_scr, mixc_scr,
                  *, tile, tiles_per_seq, n_tiles):
    step = pl.program_id(0)
    always = flag_ref[0] > 0

    @pl.when(step == 0)
    def _():
        h1_scr[...] = jnp.zeros_like(h1_scr)

    ti = jnp.minimum(step, n_tiles - 1)
    first = (ti % tiles_per_seq) == 0
    first_i = first.astype(jnp.int32)
    nblk = tile // BLOCK
    zeros_k = jnp.zeros((HEAD_DIM, 3 * BLOCK), BF16)
    d_ff = wg_ref.shape[1]

    def ffn_start():
        h = h1_scr[...]
        acc_scr[...] = h
        hn_scr[...] = _rms(h, fng_ref[...]).astype(BF16)

    def tie(x, tokens):
        if not tokens:
            return x
        tok = functools.reduce(lambda a, b: a + b, tokens)
        r, c = x.shape
        tail = jnp.where(always, x[r - SUBLANES:, 0:LANES], tok)
        left = jnp.concatenate([x[:r - SUBLANES, 0:LANES], tail], axis=0)
        return jnp.concatenate([left, x[:, LANES:]], axis=1)

    def token(y):
        r, c = y.shape
        parts = [y[i:i + SUBLANES, j:j + LANES] for i in range(0, r, SUBLANES) for j in range(0, c, LANES)]
        return functools.reduce(lambda a, b: a + b, parts)

    def ffn_chunk(c, tokens):
        cols = slice(c * FF_CHUNK, (c + 1) * FF_CHUNK)
        hn = hn_scr[...]
        gate = jnp.dot(hn, wg_ref[:, cols], preferred_element_type=F32)
        up = jnp.dot(hn, wu_ref[:, cols], preferred_element_type=F32)
        act = (gate * jax.nn.sigmoid(gate) * tie(up, tokens)).astype(BF16)
        acc_scr[...] += jnp.dot(act, wd_ref[cols, :], preferred_element_type=F32)
        return gate[0:1, 0:LANES]

    def ffn_finish():
        o_ref[...] = _rms(acc_scr[...], fg_ref[...])

    def attn_scores(b, g, start):
        rows = slice(b * BLOCK, (b + 1) * BLOCK)
        kt_prev = ktp_ref[0] if b == 0 else ktc_ref[b - 1]
        kt_cur = ktc_ref[b]
        kt_meta = ktm_ref[0]
        d = slice(g * HEAD_DIM, (g + 1) * HEAD_DIM)
        kt_g = jnp.concatenate([kt_prev[d], kt_cur[d], kt_meta[d]], axis=1)
        kt_ext = jnp.concatenate(
            [jnp.concatenate([kt_g, zeros_k], axis=1),
             jnp.concatenate([zeros_k, kt_g], axis=1)], axis=0)
        q2 = jnp.concatenate(
            [q_ref[rows, (2 * g) * LANES:(2 * g + 1) * LANES],
             q_ref[rows, (2 * g + 1) * LANES:(2 * g + 2) * LANES]], axis=0)
        q2 = jnp.where(always, q2, jnp.broadcast_to(start.astype(BF16), q2.shape))
        s = jnp.dot(q2, kt_ext, preferred_element_type=F32)
        if b == 0:
            s = s + bias_ref[first_i, g]
        else:
            s = s + bias_ref[0, g]
        p_rows = []
        rdens = []
        for cb in range(2):
            p_cols = []
            for par in range(2):
                sq = s[cb * BLOCK:(cb + 1) * BLOCK, par * 3 * BLOCK:(par + 1) * 3 * BLOCK]
                sink = sinks_ref[4 * g + 2 * cb + par]
                m = jnp.maximum(jnp.max(sq, axis=-1, keepdims=True), sink)
                e = jnp.exp(sq - m)
                den = jnp.sum(e, axis=-1, keepdims=True) + jnp.exp(sink - m)
                rden = 1.0 / den
                rdens.append(rden)
                p_cols.append((e * rden).astype(BF16))
            p_rows.append(jnp.concatenate(p_cols, axis=1))
        done = jnp.sum(functools.reduce(lambda a, b: a + b, rdens), axis=0, keepdims=True)
        return jnp.concatenate(p_rows, axis=0), done

    def attn_values(b, g, p, start):
        rows = slice(b * BLOCK, (b + 1) * BLOCK)
        if b == 0:
            v_prev = vp_ref[...]
        else:
            v_prev = vc_ref[(b - 1) * BLOCK:b * BLOCK, :]
        v_cur = vc_ref[rows, :]
        v_meta = vm_ref[...]
        ve = slice((2 * g) * LANES, (2 * g + 1) * LANES)
        vo = slice((2 * g + 1) * LANES, (2 * g + 2) * LANES)
        v_ext = jnp.concatenate(
            [v_prev[:, ve], v_cur[:, ve], v_meta[:, ve],
             v_prev[:, vo], v_cur[:, vo], v_meta[:, vo]], axis=0)
        v_ext = jnp.where(always, v_ext, jnp.broadcast_to(start.astype(BF16), v_ext.shape))
        o2 = jnp.dot(p, v_ext, preferred_element_type=F32)
        oattn_scr[rows, (2 * g) * LANES:(2 * g + 1) * LANES] = o2[:BLOCK]
        oattn_scr[rows, (2 * g + 1) * LANES:(2 * g + 2) * LANES] = o2[BLOCK:]
        return token(o2)

    n_lc = CONV_WIDTH // LANES
    off0 = CONV_HALO - (CONV_K - 1)

    def conv_start():
        halo = jnp.where(first, um_ref[...], up_ref[...])
        for lc in range(n_lc):
            lanes = slice(lc * LANES, (lc + 1) * LANES)
            ubuf_scr[lc, 0:CONV_HALO, :] = halo[:, lanes]
            ubuf_scr[lc, CONV_HALO:, :] = uc_ref[:, lanes]

    def conv_unit(rc, start):
        r0 = rc * CONV_ROWS
        ys = []
        for lc in range(n_lc):
            lanes = slice(lc * LANES, (lc + 1) * LANES)
            acc = jnp.where(always, jnp.broadcast_to(cb_ref[:, lanes], (CONV_ROWS, LANES)),
                            jnp.broadcast_to(start, (CONV_ROWS, LANES)))
            for j in range(CONV_K):
                acc = acc + cw_ref[j:j + 1, lanes] * ubuf_scr[lc, r0 + off0 + j:r0 + off0 + j + CONV_ROWS, :]
            ys.append(acc)
            start = token(acc)[0:1, :]
        y = jnp.concatenate(ys, axis=1)
        mu = jnp.mean(y, axis=-1, keepdims=True)
        yc = y - mu
        var = jnp.mean(yc * yc, axis=-1, keepdims=True)
        yn = yc * lax.rsqrt(var + NORM_EPS) * lng_ref[...] + lnb_ref[...]
        oc = yn * jax.nn.sigmoid(yn)
        ocn = _rms(oc, cog_ref[...])
        mixc_scr[r0:r0 + CONV_ROWS, :] = ocn.astype(BF16)
        return token(ocn)

    def out_proj():
        mixa = _rms(oattn_scr[...], ag_ref[...]).astype(BF16)
        h = x_ref[...]
        h = h + jnp.dot(mixa, wout_ref[0:ATTN_WIDTH, :], preferred_element_type=F32)
        h = h + jnp.dot(mixc_scr[...], wout_ref[ATTN_WIDTH:, :], preferred_element_type=F32)
        h1_scr[...] = h

    conv_start()
    n_fc = d_ff // FF_CHUNK
    n_units = nblk * N_KV_HEADS
    assert n_units == tile // CONV_ROWS and n_units + 1 <= n_fc
    probs = {}
    pending = {}
    ffn_start()
    for c in range(n_fc):
        start = ffn_chunk(c, pending.pop(c, []))
        toks = []
        if c >= 1 and c - 1 < n_units:
            toks.append(attn_values(*divmod(c - 1, N_KV_HEADS), probs.pop(c - 1), start))
        if c < n_units:
            probs[c], softmax_done = attn_scores(*divmod(c, N_KV_HEADS), start)
            toks.append(conv_unit(c, softmax_done))
        if toks:
            pending.setdefault(min(c + TIE_SKEW, n_fc - 1), []).extend(toks)
    assert not probs and not pending
    ffn_finish()
    out_proj()


def _layer(flag, sinks, q, kt, v, u, ktm, vm, um, x2d, bias, cw, cb, lng, lnb, ag, cog, wout,
           fng, wg, wu, wd, fg, tile, seq):
    n = x2d.shape[0]
    n_tiles = n // tile
    tps = seq // tile
    nblk = tile // BLOCK
    d_ff = wg.shape[1]
    const2 = lambda i: (0, 0)
    cur = lambda i: jnp.minimum(i, n_tiles - 1)
    prev_blk = lambda i: jnp.maximum(cur(i) * nblk - 1, 0)
    once = dict(pipeline_mode=pl.Buffered(1))
    return pl.pallas_call(
        functools.partial(_layer_kernel, tile=tile, tiles_per_seq=tps, n_tiles=n_tiles),
        grid=(n_tiles + 1,),
        in_specs=[
            pl.BlockSpec(memory_space=pltpu.SMEM),
            pl.BlockSpec(memory_space=pltpu.SMEM),
            pl.BlockSpec((tile, ATTN_WIDTH), lambda i: (cur(i), 0)),
            pl.BlockSpec((nblk, KV_WIDTH, BLOCK), lambda i: (cur(i), 0, 0)),
            pl.BlockSpec((1, KV_WIDTH, BLOCK), lambda i: (prev_blk(i), 0, 0)),
            pl.BlockSpec((1, KV_WIDTH, BLOCK), lambda i: (0, 0, 0)),
            pl.BlockSpec((tile, 4 * LANES), lambda i: (cur(i), 0)),
            pl.BlockSpec((BLOCK, 4 * LANES), lambda i: (prev_blk(i), 0)),
            pl.BlockSpec((BLOCK, 4 * LANES), const2),
            pl.BlockSpec((tile, CONV_WIDTH), lambda i: (cur(i), 0)),
            pl.BlockSpec((CONV_HALO, CONV_WIDTH),
                         lambda i: (jnp.maximum(cur(i) * (tile // CONV_HALO) - 1, 0), 0)),
            pl.BlockSpec((CONV_HALO, CONV_WIDTH),
                         lambda i: (BLOCK // CONV_HALO - 1, 0)),
            pl.BlockSpec((tile, D_MODEL), lambda i: (cur(i), 0)),
            pl.BlockSpec((2, N_KV_HEADS, 2 * BLOCK, 6 * BLOCK), lambda i: (0, 0, 0, 0), **once),
            pl.BlockSpec((CONV_K, CONV_WIDTH), const2),
            pl.BlockSpec((1, CONV_WIDTH), const2),
            pl.BlockSpec((1, CONV_WIDTH), const2),
            pl.BlockSpec((1, CONV_WIDTH), const2),
            pl.BlockSpec((1, ATTN_WIDTH), const2),
            pl.BlockSpec((1, CONV_WIDTH), const2),
            pl.BlockSpec((D_MODEL, D_MODEL), const2, **once),
            pl.BlockSpec((1, D_MODEL), const2),
            pl.BlockSpec((D_MODEL, d_ff), const2, **once),
            pl.BlockSpec((D_MODEL, d_ff), const2, **once),
            pl.BlockSpec((d_ff, D_MODEL), const2, **once),
            pl.BlockSpec((1, D_MODEL), const2),
        ],
        out_specs=pl.BlockSpec((tile, D_MODEL), lambda i: (jnp.maximum(i - 1, 0), 0)),
        out_shape=jax.ShapeDtypeStruct((n, D_MODEL), F32),
        scratch_shapes=[
            pltpu.VMEM((tile, D_MODEL), F32),
            pltpu.VMEM((tile, D_MODEL), BF16),
            pltpu.VMEM((tile, D_MODEL), F32),
            pltpu.VMEM((tile, ATTN_WIDTH), F32),
            pltpu.VMEM((CONV_WIDTH // LANES, CONV_HALO + tile, LANES), F32),
            pltpu.VMEM((tile, CONV_WIDTH), BF16),
        ],
        compiler_params=pltpu.CompilerParams(
            dimension_semantics=("arbitrary",), vmem_limit_bytes=VMEM_LIMIT),
        name="layer",
    )(flag, sinks, q, kt, kt, ktm, v, v, vm, u, u, um, x2d, bias, cw, cb, lng, lnb, ag, cog, wout,
      fng, wg, wu, wd, fg)


def kernel(x, meta_tokens, attn_norm_g, w_in, attn_sinks, conv_w, conv_b, conv_ln_g,
           conv_ln_b, attn_out_g, conv_out_g, w_out, ffn_norm_g, w_gate, w_up, w_down,
           final_norm_g):
    batch, seq, d = x.shape
    assert d == D_MODEL and seq % TILE == 0 and seq % TILE_IN == 0
    assert attn_norm_g.shape[0] == 1, "single layer"
    assert w_gate.shape[2] % FF_CHUNK == 0
    row = lambda a: a.reshape(1, -1).astype(F32)
    x2d = x.reshape(batch * seq, d)
    g_in = row(attn_norm_g[0])

    q, kt, v, u, w_in_b, w_out_b, w_gate_b, w_up_b, w_down_b = _inproj(
        x2d, g_in, w_in[0], w_out[0], w_gate[0], w_up[0], w_down[0], TILE_IN)
    meta_pad = jnp.pad(meta_tokens.astype(F32), ((BLOCK - N_META, 0), (0, 0)))
    _, ktm, vm, um = _inproj_meta(meta_pad, g_in, w_in_b)

    always = jnp.ones((1,), jnp.int32)
    out = _layer(always, attn_sinks[0].astype(F32), q, kt, v, u, ktm, vm, um, x2d, _attn_bias(),
                 conv_w[0].astype(F32), row(conv_b[0]), row(conv_ln_g[0]), row(conv_ln_b[0]),
                 row(attn_out_g[0]), row(conv_out_g[0]), w_out_b,
                 row(ffn_norm_g[0]), w_gate_b, w_up_b, w_down_b, row(final_norm_g), TILE, seq)
    return out.reshape(batch, seq, d)
```
